```python
import jax
import jax.numpy as jnp
from jax import lax
import numpy as np

D_MODEL = 1024
BATCH = 4
SEQ = 8192
DEPTH = 4

CTX_LEN = 256
GRID_W = 64
D_MIX = D_MODEL
CONV_CH = D_MIX // 2
RET_HEADS = 4
RET_DK = (D_MIX - CONV_CH) // RET_HEADS
RET_DV = RET_DK
RET_W = RET_HEADS * RET_DV
N_PROJ = 3 * CONV_CH + 4 * RET_W
CHUNK = 128
ROPE_BASE = 10000.0
PEER_HEADS = 8
N_KEYS = 128
N_EXPERTS = N_KEYS * N_KEYS
PEER_TOPK = 16
PEER_DK = 256
PEER_HALF = PEER_DK // 2
PEER_BLOCK = 128
ALPHA = float((2 * DEPTH) ** 0.25)
BETA = float((8 * DEPTH) ** -0.25)
LN_EPS = 1e-5
GN_EPS = 1e-5

kernel_name = 'hybrid_conv_retention_peer_dit'


def _layer_norm(x, g, b):
    xf = x.astype(jnp.float32)
    mu = jnp.mean(xf, -1, keepdims=True)
    var = jnp.mean(jnp.square(xf - mu), -1, keepdims=True)
    return ((xf - mu) * lax.rsqrt(var + LN_EPS)).astype(x.dtype) * g + b


def _modulate(x, shift, scale):
    return x * (1 + scale) + shift


def _conv3_grid(z, w):
    b, length, ch = z.shape
    rows = length // GRID_W
    zp = jnp.pad(z.reshape(b, rows, GRID_W, ch), ((0, 0), (0, 0), (1, 1), (0, 0)))
    y = zp[:, :, :-2] * w[:, 0] + zp[:, :, 1:-1] * w[:, 1] + zp[:, :, 2:] * w[:, 2]
    return y.reshape(b, length, ch)


def _conv3_seq(z, w):
    zp = jnp.pad(z, ((0, 0), (1, 1), (0, 0)))
    return zp[:, :-2] * w[:, 0] + zp[:, 1:-1] * w[:, 1] + zp[:, 2:] * w[:, 2]


def _rope_2d(t):
    length = t.shape[1]
    pos = jnp.arange(length)
    nf = RET_DK // 4
    inv = ROPE_BASE ** (-jnp.arange(nf, dtype=jnp.float32) / nf)

    def rot(u, p):
        ang = p.astype(jnp.float32)[:, None] * inv[None, :]
        cos = jnp.cos(ang)[None, :, None, :].astype(u.dtype)
        sin = jnp.sin(ang)[None, :, None, :].astype(u.dtype)
        u1, u2 = u[..., :nf], u[..., nf:]
        return jnp.concatenate([u1 * cos - u2 * sin, u1 * sin + u2 * cos], -1)

    half = RET_DK // 2
    return jnp.concatenate([rot(t[..., :half], pos // GRID_W), rot(t[..., half:], pos % GRID_W)], -1)


def _heads(t):
    return t.reshape(t.shape[:-1] + (RET_HEADS, RET_DK))


def _bhld(t):
    return jnp.transpose(t, (0, 2, 1, 3)).astype(jnp.float32)


def _log_gammas(offset):
    h = jnp.arange(RET_HEADS, dtype=jnp.float32)
    return jnp.log1p(-jnp.exp2(-(5.0 + offset + h)))


def _final_state(k, v, lg):
    length = k.shape[2]
    w = jnp.exp((length - 1 - jnp.arange(length, dtype=jnp.float32))[None, :] * lg[:, None])
    return jnp.einsum('bhld,bhle->bhde', k * w[None, :, :, None], v)


def _chunk_retention(q, k, v, lg, state0, inclusive):
    b, h, length, dk = q.shape
    dv = v.shape[-1]
    n = length // CHUNK
    qc = q.reshape(b, h, n, CHUNK, dk)
    kc = k.reshape(b, h, n, CHUNK, dk)
    vc = v.reshape(b, h, n, CHUNK, dv)
    pos = jnp.arange(CHUNK, dtype=jnp.float32)
    diff = pos[:, None] - pos[None, :]
    mask = diff >= 0 if inclusive else diff > 0
    dmat = jnp.where(mask[None], jnp.exp(jnp.where(mask, diff, 0.0)[None] * lg[:, None, None]), 0.0)
    scores = jnp.einsum('bhncd,bhnsd->bhncs', qc, kc) * dmat[None, :, None]
    o_intra = jnp.einsum('bhncs,bhnse->bhnce', scores, vc)
    k_w = kc * jnp.exp((CHUNK - 1 - pos)[None, :] * lg[:, None])[None, :, None, :, None]
    upd = jnp.einsum('bhnsd,bhnse->bhnde', k_w, vc)
    chunk_decay = jnp.exp(CHUNK * lg)[None, :, None, None]

    def step(s, u):
        return chunk_decay * s + u, s

    _, s_prev = lax.scan(step, state0, jnp.moveaxis(upd, 2, 0))
    s_prev = jnp.moveaxis(s_prev, 0, 2)
    q_w = qc * jnp.exp((pos + 1)[None, :] * lg[:, None])[None, :, None, :, None]
    o_cross = jnp.einsum('bhncd,bhnde->bhnce', q_w, s_prev)
    return (o_intra + o_cross).reshape(b, h, length, dv)


def _retention_bidir(q, k, v, s_f, s_b):
    o_f = _chunk_retention(q, k, v, _log_gammas(0.0), s_f, True)
    o_b = jnp.flip(_chunk_retention(jnp.flip(q, 2), jnp.flip(k, 2), jnp.flip(v, 2),
                                    _log_gammas(0.5), s_b, False), 2)
    return o_f + o_b


def _retention_out(o, g):
    mu = jnp.mean(o, -1, keepdims=True)
    var = jnp.mean(jnp.square(o - mu), -1, keepdims=True)
    on = (o - mu) * lax.rsqrt(var + GN_EPS)
    b, h, length, dv = o.shape
    on = jnp.transpose(on, (0, 2, 1, 3)).reshape(b, length, h * dv).astype(g.dtype)
    return jax.nn.silu(g) * on


def _token_mixer(h_lat, h_ctx, w_in, conv_w, w_out, with_ctx_out):
    scale_k = RET_DK ** -0.5
    bg, cg, hx, q, k, v, g = jnp.split(h_lat @ w_in, 7, axis=-1)
    conv_lat = bg * _conv3_grid(cg * hx, conv_w)
    q_l = _bhld(_rope_2d(_heads(q)))
    k_l = _bhld(_rope_2d(_heads(k))) * scale_k
    v_l = _bhld(_heads(v))
    k0 = 3 * CONV_CH + RET_W
    if with_ctx_out:
        bgc, cgc, hxc, qc, kc, vc, gc = jnp.split(h_ctx @ w_in, 7, axis=-1)
    else:
        kc, vc = jnp.split(h_ctx @ w_in[:, k0:k0 + 2 * RET_W], 2, axis=-1)
    k_c = _bhld(_heads(kc)) * scale_k
    v_c = _bhld(_heads(vc))
    s_f = _final_state(k_c, v_c, _log_gammas(0.0))
    s_b = _final_state(jnp.flip(k_c, 2), jnp.flip(v_c, 2), _log_gammas(0.5))
    ret_lat = _retention_out(_retention_bidir(q_l, k_l, v_l, s_f, s_b), g)
    y_lat = jnp.concatenate([conv_lat, ret_lat], -1) @ w_out
    if not with_ctx_out:
        return y_lat, None
    conv_ctx = bgc * _conv3_seq(cgc * hxc, conv_w)
    zero = jnp.zeros_like(s_f)
    ret_ctx = _retention_out(_retention_bidir(_bhld(_heads(qc)), k_c, v_c, zero, zero), gc)
    y_ctx = jnp.concatenate([conv_ctx, ret_ctx], -1) @ w_out
    return y_lat, y_ctx


def _peer_ffn(h, wq, keys, u, v):
    b, length, d = h.shape
    blocks = h.reshape(-1, PEER_BLOCK, d)

    def one(hb):
        q = (hb @ wq).reshape(PEER_BLOCK, PEER_HEADS, 2, PEER_HALF)
        s = jnp.einsum('thpd,pnd->thpn', q, keys).astype(jnp.float32)
        sv, si = lax.top_k(s, PEER_TOPK)
        cand = sv[:, :, 0, :, None] + sv[:, :, 1, None, :]
        cv, ci = lax.top_k(cand.reshape(PEER_BLOCK, PEER_HEADS, PEER_TOPK * PEER_TOPK), PEER_TOPK)
        i1 = jnp.take_along_axis(si[:, :, 0], ci // PEER_TOPK, axis=-1)
        i2 = jnp.take_along_axis(si[:, :, 1], ci % PEER_TOPK, axis=-1)
        e = i1 * N_KEYS + i2
        gate = jax.nn.softmax(cv, axis=-1)
        act = jax.nn.gelu(jnp.einsum('thkd,td->thk', jnp.take(u, e, axis=0), hb), approximate=False)
        return jnp.einsum('thk,thkd->td', (gate * act).astype(hb.dtype), jnp.take(v, e, axis=0))

    return lax.map(one, blocks).reshape(b, length, d)


def setup_inputs(seed: int = 0) -> dict:
    key = jax.random.key(seed)
    ks = jax.random.split(key, 17)
    n = jax.random.normal
    f32 = jnp.float32
    col_scale = jnp.concatenate([
        jnp.ones((2 * CONV_CH,), f32), jnp.full((CONV_CH,), BETA, f32),
        jnp.ones((2 * RET_W,), f32), jnp.full((RET_W,), BETA, f32),
        jnp.ones((RET_W,), f32)])
    return {
        'x': n(ks[0], (BATCH, SEQ, D_MODEL), f32),
        'c': n(ks[1], (BATCH, D_MODEL), f32),
        'ctx': n(ks[2], (BATCH, CTX_LEN, D_MODEL), f32),
        'c_ctx': n(ks[3], (D_MODEL,), f32),
        'w_mod': n(ks[4], (DEPTH, D_MODEL, 6 * D_MODEL), f32) * D_MODEL ** -0.5,
        'b_mod': 0.02 * n(ks[5], (DEPTH, 6 * D_MODEL), f32),
        'w_in': n(ks[6], (DEPTH, D_MODEL, N_PROJ), f32) * (D_MODEL ** -0.5) * col_scale,
        'conv_w': n(ks[7], (DEPTH, CONV_CH, 3), f32) * 3 ** -0.5,
        'w_out': n(ks[8], (DEPTH, D_MIX, D_MODEL), f32) * (D_MIX ** -0.5) * BETA,
        'ln1_g': 1.0 + 0.02 * n(ks[9], (DEPTH, D_MODEL), f32),
        'ln1_b': 0.02 * n(ks[10], (DEPTH, D_MODEL), f32),
        'peer_wq': n(ks[11], (DEPTH, D_MODEL, PEER_HEADS * PEER_DK), f32) * D_MODEL ** -0.5,
        'peer_keys': n(ks[12], (DEPTH, 2, N_KEYS, PEER_HALF), f32) * PEER_HALF ** -0.5,
        'peer_u': n(ks[13], (DEPTH, N_EXPERTS, D_MODEL), f32) * D_MODEL ** -0.5,
        'peer_v': n(ks[14], (DEPTH, N_EXPERTS, D_MODEL), f32) * BETA,
        'ln2_g': 1.0 + 0.02 * n(ks[15], (DEPTH, D_MODEL), f32),
        'ln2_b': 0.02 * n(ks[16], (DEPTH, D_MODEL), f32),
    }


def reference(x, c, ctx, c_ctx, w_mod, b_mod, w_in, conv_w, w_out, ln1_g, ln1_b,
              peer_wq, peer_keys, peer_u, peer_v, ln2_g, ln2_b):
    sc = jax.nn.silu(c)
    scc = jax.nn.silu(c_ctx)
    x_lat, x_ctx = x, ctx
    for l in range(DEPTH):
        last = l == DEPTH - 1
        m_lat = [m[:, None, :] for m in jnp.split(sc @ w_mod[l] + b_mod[l], 6, axis=-1)]
        m_ctx = jnp.split(scc @ w_mod[l] + b_mod[l], 6, axis=-1)
        h_lat = _modulate(x_lat, m_lat[0], m_lat[1])
        h_ctx = _modulate(x_ctx, m_ctx[0], m_ctx[1])
        y_lat, y_ctx = _token_mixer(h_lat, h_ctx, w_in[l], conv_w[l], w_out[l], not last)
        x_lat = _layer_norm(ALPHA * x_lat + m_lat[2] * y_lat, ln1_g[l], ln1_b[l])
        h2 = _modulate(x_lat, m_lat[3], m_lat[4])
        f_lat = _peer_ffn(h2, peer_wq[l], peer_keys[l], peer_u[l], peer_v[l])
        x_lat = _layer_norm(ALPHA * x_lat + m_lat[5] * f_lat, ln2_g[l], ln2_b[l])
        if not last:
            x_ctx = _layer_norm(ALPHA * x_ctx + m_ctx[2] * y_ctx, ln1_g[l], ln1_b[l])
            h2c = _modulate(x_ctx, m_ctx[3], m_ctx[4])
            f_ctx = _peer_ffn(h2c, peer_wq[l], peer_keys[l], peer_u[l], peer_v[l])
            x_ctx = _layer_norm(ALPHA * x_ctx + m_ctx[5] * f_ctx, ln2_g[l], ln2_b[l])
    return x_lat
```

```python
import functools
import math

import jax
import jax.numpy as jnp
from jax import lax
from jax.experimental import pallas as pl
from jax.experimental.pallas import tpu as pltpu

F32 = jnp.float32
BF16 = jnp.bfloat16

GRID_W = 64
CONV_CH = 512
RET_HEADS = 4
RET_DK = 128
RET_W = RET_HEADS * RET_DK
CHUNK = 128
ROPE_BASE = 10000.0
PEER_HEADS = 8
N_KEYS = 128
PEER_TOPK = 16
PEER_HALF = 128
LN_EPS = 1e-5
GN_EPS = 1e-5

TILE = 256
EXPERT_BLOCK = 1024
VMEM_LIMIT = 56 * 1024 * 1024

NEG_INF = float("-inf")
INV_SQRT2 = 0.7071067811865476


def _bdot(a, b):
    return jnp.dot(a.astype(BF16), b.astype(BF16), preferred_element_type=F32)


def _bdot_nt(a, b):
    return lax.dot_general(a.astype(BF16), b.astype(BF16), (((1,), (1,)), ((), ())),
                           preferred_element_type=F32)


def _layer_norm(x, g, b):
    mu = jnp.mean(x, axis=-1, keepdims=True)
    xc = x - mu
    var = jnp.mean(xc * xc, axis=-1, keepdims=True)
    return xc * lax.rsqrt(var + LN_EPS) * g + b


def _silu(x):
    return x * (1.0 / (1.0 + jnp.exp(-x)))


def _mod_kernel(cc_ref, w_ref, b_ref, o_ref):
    a = _silu(cc_ref[...])
    o_ref[0] = _bdot(a, w_ref[0]) + b_ref[0]


def _modulations(cc, w_mod, b_mod):
    depth, d, d6 = w_mod.shape
    nblk = d6 // d
    return pl.pallas_call(
        _mod_kernel,
        grid=(depth, nblk),
        in_specs=[
            pl.BlockSpec((8, d), lambda l, j: (0, 0)),
            pl.BlockSpec((1, d, d), lambda l, j: (l, 0, j)),
            pl.BlockSpec((1, 1, d), lambda l, j: (l, 0, j)),
        ],
        out_specs=pl.BlockSpec((1, 8, d), lambda l, j: (l, 0, j)),
        out_shape=jax.ShapeDtypeStruct((depth, 8, d6), F32),
        name="adaln_mod",
    )(cc, w_mod, b_mod.reshape(depth, 1, d6))


def _retention_tile(q, k, v, dmat_ref, qdec, kdec, cd_ref, state_ref, chunk_order):
    t = q.shape[0]
    outs = [[None] * RET_HEADS for _ in range(t // CHUNK)]
    for c in chunk_order:
        rows = slice(c * CHUNK, (c + 1) * CHUNK)
        for hd in range(RET_HEADS):
            cols = slice(hd * RET_DK, (hd + 1) * RET_DK)
            qc, kc, vc = q[rows, cols], k[rows, cols], v[rows, cols]
            scores = _bdot_nt(qc, kc) * dmat_ref[hd]
            state = state_ref[hd]
            o = _bdot(scores, vc) + _bdot(qc * qdec[:, cols], state)
            kw = (kc * kdec[:, cols]).T
            state_ref[hd] = cd_ref[hd] * state + _bdot(kw, vc)
            outs[c][hd] = o
    return jnp.concatenate([jnp.concatenate(r, axis=1) for r in outs], axis=0)


def _mixer_a_kernel(x_ref, mod_ref, w_in_ref, convw_ref, cos_ref, sin_ref,
                    dmat_ref, qdec_ref, kdec_ref, cd_ref,
                    q_ref, k_ref, v_ref, sg_ref, conv_ref, ob_ref, state_ref):
    step = pl.program_id(1)

    @pl.when(step == 0)
    def _():
        state_ref[...] = jnp.zeros_like(state_ref)

    t = x_ref.shape[0]
    x = x_ref[...]
    h = (x * (1.0 + mod_ref[1:2, :]) + mod_ref[0:1, :]).astype(BF16)

    def proj(s):
        return jnp.dot(h, w_in_ref[:, s * CONV_CH:(s + 1) * CONV_CH], preferred_element_type=F32)

    bg, cg, hx = proj(0), proj(1), proj(2)
    z = cg * hx
    row = lax.broadcasted_iota(jnp.int32, (t, CONV_CH), 0)
    pmask = jnp.where(step == 0, t - 1, GRID_W - 1)
    pos = row & pmask
    zl = jnp.where(pos == 0, 0.0, pltpu.roll(z, 1, 0))
    zr = jnp.where(pos == pmask, 0.0, pltpu.roll(z, t - 1, 0))
    y = zl * convw_ref[0:1, :] + z * convw_ref[1:2, :] + zr * convw_ref[2:3, :]
    conv_ref[...] = bg * y

    cos = jnp.concatenate([cos_ref[...]] * RET_HEADS, axis=1)
    sin = jnp.concatenate([sin_ref[...]] * RET_HEADS, axis=1)
    lane = lax.broadcasted_iota(jnp.int32, (t, RET_W), 1)
    first = (lane & (RET_DK // 2 - 1)) < (RET_DK // 4)

    def rope(u):
        swapped = jnp.where(first, pltpu.roll(u, RET_W - RET_DK // 4, 1), pltpu.roll(u, RET_DK // 4, 1))
        return u * cos + swapped * sin

    q = rope(proj(3))
    k = rope(proj(4)) * (RET_DK ** -0.5)
    v = proj(5)
    g = proj(6)
    q_ref[...] = q
    k_ref[...] = k
    v_ref[...] = v
    sg_ref[...] = _silu(g)

    nchunk = t // CHUNK
    ob_ref[...] = _retention_tile(q, k, v, dmat_ref, qdec_ref[...], kdec_ref[...], cd_ref,
                                  state_ref, list(reversed(range(nchunk))))


def _mixer_a(xs, mods, w_in, convw, cos_t, sin_t, dmat, qdec, kdec, cd):
    b, s, d = xs.shape
    nt = s // TILE

    def tile_rev(i):
        return jnp.where(i == 0, 0, nt - i)

    tok = lambda w: pl.BlockSpec((None, TILE, w), lambda bi, i: (bi, tile_rev(i), 0))
    full = lambda a: pl.BlockSpec(a.shape, lambda bi, i: (0,) * a.ndim)
    out_sds = jax.ShapeDtypeStruct((b, s, RET_W), F32)
    return pl.pallas_call(
        _mixer_a_kernel,
        grid=(b, nt),
        in_specs=[
            tok(d),
            pl.BlockSpec((None, None, 6, d), lambda bi, i: (bi, jnp.minimum(tile_rev(i), 1), 0, 0)),
            full(w_in), full(convw),
            pl.BlockSpec((TILE, RET_DK), lambda bi, i: (tile_rev(i), 0)),
            pl.BlockSpec((TILE, RET_DK), lambda bi, i: (tile_rev(i), 0)),
            full(dmat), full(qdec), full(kdec), full(cd),
        ],
        out_specs=[tok(RET_W)] * 6,
        out_shape=[out_sds] * 6,
        scratch_shapes=[pltpu.VMEM((RET_HEADS, RET_DK, RET_DK), F32)],
        compiler_params=pltpu.CompilerParams(
            dimension_semantics=("arbitrary", "arbitrary"), vmem_limit_bytes=VMEM_LIMIT),
        name="mixer_a",
    )(xs, mods, w_in, convw, cos_t, sin_t, dmat, qdec, kdec, cd)


def _mixer_b_kernel(alpha, x_ref, mod_ref, q_ref, k_ref, v_ref, sg_ref, conv_ref, ob_ref,
                    w_out_ref, lng_ref, lnb_ref, dmat_ref, qdec_ref, kdec_ref, cd_ref,
                    o_ref, state_ref):
    step = pl.program_id(1)

    @pl.when(step == 0)
    def _():
        state_ref[...] = jnp.zeros_like(state_ref)

    t = x_ref.shape[0]
    o = _retention_tile(q_ref[...], k_ref[...], v_ref[...], dmat_ref, qdec_ref[...], kdec_ref[...],
                        cd_ref, state_ref, list(range(t // CHUNK)))
    o = o + ob_ref[...]
    sg = sg_ref[...]
    rets = []
    for hd in range(RET_HEADS):
        cols = slice(hd * RET_DK, (hd + 1) * RET_DK)
        oh = o[:, cols]
        mu = jnp.mean(oh, axis=-1, keepdims=True)
        oc = oh - mu
        var = jnp.mean(oc * oc, axis=-1, keepdims=True)
        rets.append(sg[:, cols] * (oc * lax.rsqrt(var + GN_EPS)))
    ret = jnp.concatenate(rets, axis=1)
    y = _bdot(conv_ref[...], w_out_ref[0:CONV_CH, :]) + _bdot(ret, w_out_ref[CONV_CH:, :])
    xr = alpha * x_ref[...] + mod_ref[2:3, :] * y
    o_ref[...] = _layer_norm(xr, lng_ref[...], lnb_ref[...])


def _mixer_b(alpha, xs, mods, q, k, v, sg, conv, ob, w_out, lng, lnb, dmat, qdec, kdec, cd):
    b, s, d = xs.shape
    nt = s // TILE
    tok = lambda w: pl.BlockSpec((None, TILE, w), lambda bi, i: (bi, i, 0))
    full = lambda a: pl.BlockSpec(a.shape, lambda bi, i: (0,) * a.ndim)
    return pl.pallas_call(
        functools.partial(_mixer_b_kernel, alpha),
        grid=(b, nt),
        in_specs=[
            tok(d),
            pl.BlockSpec((None, None, 6, d), lambda bi, i: (bi, jnp.minimum(i, 1), 0, 0)),
            tok(RET_W), tok(RET_W), tok(RET_W), tok(RET_W), tok(RET_W), tok(RET_W),
            full(w_out), full(lng), full(lnb), full(dmat), full(qdec), full(kdec), full(cd),
        ],
        out_specs=tok(d),
        out_shape=jax.ShapeDtypeStruct((b, s, d), F32),
        scratch_shapes=[pltpu.VMEM((RET_HEADS, RET_DK, RET_DK), F32)],
        compiler_params=pltpu.CompilerParams(
            dimension_semantics=("arbitrary", "arbitrary"), vmem_limit_bytes=VMEM_LIMIT),
        name="mixer_b",
    )(xs, mods, q, k, v, sg, conv, ob, w_out, lng, lnb, dmat, qdec, kdec, cd)


_NRANK = PEER_TOPK + 1
_PAIRS = [(a, b) for a in range(_NRANK) for b in range(_NRANK) if (a + 1) * (b + 1) <= _NRANK]


def _tree(fn, xs):
    xs = list(xs)
    while len(xs) > 1:
        nxt = [fn(xs[i], xs[i + 1]) for i in range(0, len(xs) - 1, 2)]
        if len(xs) % 2:
            nxt.append(xs[-1])
        xs = nxt
    return xs[0]


def _peer_select(x1_ref, mod_ref, wq_ref, keys_ref, h2_ref, s3_ref, sv_ref,
                 s2_ref, b2_ref, a1_ref, cc_ref, acc_ref):
    x1 = x1_ref[...]
    h2 = (x1 * (1.0 + mod_ref[4:5, :]) + mod_ref[3:4, :]).astype(BF16)
    h2_ref[...] = h2
    for hd in range(PEER_HEADS):
        qh = jnp.dot(h2, wq_ref[:, hd * 2 * PEER_HALF:(hd + 1) * 2 * PEER_HALF],
                     preferred_element_type=F32)
        for p in range(2):
            st = _bdot_nt(keys_ref[p], qh[:, p * PEER_HALF:(p + 1) * PEER_HALF])
            s3_ref[p, :, hd, :] = st
            if p == 0:
                cc_ref[hd] = st
            else:
                s2_ref[hd] = st

    for p in range(2):
        m0 = jnp.max(s3_ref[p], axis=0)
        sv_ref[p, 0] = m0

        def body(r, m, p=p):
            w = s3_ref[p]
            w = jnp.where(w == m[None], NEG_INF, w)
            s3_ref[p] = w
            mn = jnp.max(w, axis=0)
            sv_ref[p, pl.ds(r, 1)] = mn[None]
            return mn

        lax.fori_loop(1, _NRANK, body, m0)

    cands = [sv_ref[0, a] + sv_ref[1, b] for a, b in _PAIRS]
    cur = cands
    for r in range(PEER_TOPK + 1):
        mx = _tree(jnp.maximum, cur)
        if r == PEER_TOPK - 1:
            c_last = mx
        if r < PEER_TOPK:
            cur = [jnp.where(c == mx, NEG_INF, c) for c in cur]
    thr = 0.5 * (c_last + mx)
    mtot = cands[0]
    z = _tree(jnp.add, [jnp.where(c >= thr, jnp.exp(c - mtot), 0.0) for c in cands])
    inv_z = 1.0 / z
    m1 = sv_ref[0, 0]
    m2 = sv_ref[1, 0]
    for hd in range(PEER_HEADS):
        s1 = cc_ref[hd]
        a1_ref[hd] = jnp.exp(s1 - m1[hd:hd + 1]) * inv_z[hd:hd + 1]
        cc_ref[hd] = thr[hd:hd + 1] - s1
        b2_ref[hd] = jnp.exp(s2_ref[hd] - m2[hd:hd + 1])
    acc_ref[...] = jnp.zeros_like(acc_ref)


def _peer_kernel(alpha, x1_ref, mod_ref, wq_ref, keys_ref, u_ref, vt_ref, lng_ref, lnb_ref,
                 o_ref, h2_ref, s3_ref, sv_ref, s2_ref, b2_ref, a1_ref, cc_ref, w_ref, acc_ref):
    eb = pl.program_id(2)
    neb = pl.num_programs(2)

    @pl.when(eb == 0)
    def _():
        _peer_select(x1_ref, mod_ref, wq_ref, keys_ref, h2_ref, s3_ref, sv_ref,
                     s2_ref, b2_ref, a1_ref, cc_ref, acc_ref)

    nsub = u_ref.shape[0] // N_KEYS
    act_t = _bdot_nt(u_ref[...], h2_ref[...])
    for j in range(nsub):
        i1 = eb * nsub + j
        gate = None
        for hd in range(PEER_HEADS):
            crow = cc_ref[hd, pl.ds(i1, 1), :]
            arow = a1_ref[hd, pl.ds(i1, 1), :]
            term = arow * jnp.where(s2_ref[hd] >= crow, b2_ref[hd], 0.0)
            gate = term if gate is None else gate + term
        a = act_t[j * N_KEYS:(j + 1) * N_KEYS]
        gelu = 0.5 * a * (1.0 + lax.erf(a * INV_SQRT2))
        w_ref[j * N_KEYS:(j + 1) * N_KEYS, :] = (gate * gelu).astype(BF16)
    acc_ref[...] += jnp.dot(vt_ref[...], w_ref[...], preferred_element_type=F32)

    @pl.when(eb == neb - 1)
    def _():
        f = acc_ref[...].T
        xr = alpha * x1_ref[...] + mod_ref[5:6, :] * f
        o_ref[...] = _layer_norm(xr, lng_ref[...], lnb_ref[...])


def _peer(alpha, x1, mods, wq, keys, u, vt, lng, lnb):
    b, s, d = x1.shape
    nt = s // TILE
    n_exp = u.shape[0]
    neb = n_exp // EXPERT_BLOCK
    full = lambda a: pl.BlockSpec(a.shape, lambda bi, i, e: (0,) * a.ndim)
    return pl.pallas_call(
        functools.partial(_peer_kernel, alpha),
        grid=(b, nt, neb),
        in_specs=[
            pl.BlockSpec((None, TILE, d), lambda bi, i, e: (bi, i, 0)),
            pl.BlockSpec((None, None, 6, d), lambda bi, i, e: (bi, jnp.minimum(i, 1), 0, 0)),
            full(wq), full(keys),
            pl.BlockSpec((EXPERT_BLOCK, d), lambda bi, i, e: (e, 0)),
            pl.BlockSpec((d, EXPERT_BLOCK), lambda bi, i, e: (0, e)),
            full(lng), full(lnb),
        ],
        out_specs=pl.BlockSpec((None, TILE, d), lambda bi, i, e: (bi, i, 0)),
        out_shape=jax.ShapeDtypeStruct((b, s, d), F32),
        scratch_shapes=[
            pltpu.VMEM((TILE, d), BF16),
            pltpu.VMEM((2, N_KEYS, PEER_HEADS, TILE), F32),
            pltpu.VMEM((2, _NRANK, PEER_HEADS, TILE), F32),
            pltpu.VMEM((PEER_HEADS, N_KEYS, TILE), F32),
            pltpu.VMEM((PEER_HEADS, N_KEYS, TILE), F32),
            pltpu.VMEM((PEER_HEADS, N_KEYS, TILE), F32),
            pltpu.VMEM((PEER_HEADS, N_KEYS, TILE), F32),
            pltpu.VMEM((EXPERT_BLOCK, TILE), BF16),
            pltpu.VMEM((d, TILE), F32),
        ],
        compiler_params=pltpu.CompilerParams(
            dimension_semantics=("arbitrary", "arbitrary", "arbitrary"), vmem_limit_bytes=VMEM_LIMIT),
        name="peer",
    )(x1, mods, wq, keys, u, vt, lng, lnb)


def _rope_tables(ctx_len, seq_len):
    nf = RET_DK // 4
    inv = ROPE_BASE ** (-jnp.arange(nf, dtype=F32) / nf)
    pos = jnp.arange(seq_len)

    def cs(p):
        ang = p.astype(F32)[:, None] * inv[None, :]
        return jnp.cos(ang), jnp.sin(ang)

    cr, sr = cs(pos // GRID_W)
    cc, sc = cs(pos % GRID_W)
    cos = jnp.concatenate([cr, cr, cc, cc], axis=1)
    sin = jnp.concatenate([-sr, sr, -sc, sc], axis=1)
    cos = jnp.concatenate([jnp.ones((ctx_len, RET_DK), F32), cos], axis=0)
    sin = jnp.concatenate([jnp.zeros((ctx_len, RET_DK), F32), sin], axis=0)
    return cos, sin


def _decay_tables(offset, backward):
    hidx = jnp.arange(RET_HEADS, dtype=F32)
    lg = jnp.log1p(-jnp.exp2(-(5.0 + offset + hidx)))
    pos = jnp.arange(CHUNK, dtype=F32)
    diff = pos[:, None] - pos[None, :]
    if backward:
        mask = diff < 0
        dist = -diff
        qexp, kexp = CHUNK - pos, pos
    else:
        mask = diff >= 0
        dist = diff
        qexp, kexp = pos + 1.0, CHUNK - 1.0 - pos
    dmat = jnp.where(mask[None], jnp.exp(jnp.where(mask, dist, 0.0)[None] * lg[:, None, None]), 0.0)
    lanes = lambda e: jnp.repeat(jnp.exp(e[None, :] * lg[:, None]).T, RET_DK, axis=1)
    cd = jnp.broadcast_to(jnp.exp(CHUNK * lg)[:, None, None], (RET_HEADS, 1, RET_DK))
    return dmat, lanes(qexp), lanes(kexp), cd


def kernel(x, c, ctx, c_ctx, w_mod, b_mod, w_in, conv_w, w_out, ln1_g, ln1_b,
           peer_wq, peer_keys, peer_u, peer_v, ln2_g, ln2_b):
    batch, seq, d = x.shape
    ctx_len = ctx.shape[1]
    depth = w_mod.shape[0]
    assert ctx_len == TILE and seq % TILE == 0 and batch < 8
    assert peer_u.shape[1] == N_KEYS * N_KEYS and peer_u.shape[1] % EXPERT_BLOCK == 0
    alpha = float((2 * depth) ** 0.25)

    cc = jnp.zeros((8, d), F32).at[:batch].set(c).at[batch].set(c_ctx)
    mod_all = _modulations(cc, w_mod, b_mod).reshape(depth, 8, 6, d)
    mods = jnp.stack([jnp.broadcast_to(mod_all[:, batch][:, None], (depth, batch, 6, d)),
                      mod_all[:, :batch]], axis=2)

    cos_t, sin_t = _rope_tables(ctx_len, seq)
    dec_f = _decay_tables(0.0, False)
    dec_b = _decay_tables(0.5, True)

    w_in_b = w_in.astype(BF16)
    w_out_b = w_out.astype(BF16)
    wq_b = peer_wq.astype(BF16)
    keys_b = peer_keys.astype(BF16)
    u_b = peer_u.astype(BF16)
    vt_b = jnp.swapaxes(peer_v, 1, 2).astype(BF16)
    convw = jnp.swapaxes(conv_w, 1, 2)

    xs = jnp.concatenate([ctx, x], axis=1)
    for l in range(depth):
        q, k, v, sg, conv, ob = _mixer_a(xs, mods[l], w_in_b[l], convw[l], cos_t, sin_t, *dec_b)
        x1 = _mixer_b(alpha, xs, mods[l], q, k, v, sg, conv, ob, w_out_b[l],
                      ln1_g[l][None], ln1_b[l][None], *dec_f)
        xs = _peer(alpha, x1, mods[l], wq_b[l], keys_b[l], u_b[l], vt_b[l],
                   ln2_g[l][None], ln2_b[l][None])
    return xs[:, ctx_len:]
```
